```python
import jax, jax.numpy as jnp
from jax import lax
import numpy as np

D_MODEL = 4096
BATCH = 1
SEQ = 8192
DEPTH = 2
DEC_BATCH = 2
DEC_SEQ = 8192
PAST_LEN = 128

N_META = 16
GRID_W = 64
Q_BLOCK = 128
NORM_EPS = 1e-6
ROPE_THETA = 10000.0
D_FF = 11008
FFN_HALF = 0.5
MLA_HEADS = 32
MLA_Q_LORA = 1024
MLA_KV_LORA = 512
MLA_NOPE = 128
MLA_ROPE = 64
MLA_V = 128
GQA_Q_HEADS = 32
GQA_KV_HEADS = 8
GQA_HEAD_DIM = 128
GQA_GROUP = GQA_Q_HEADS // GQA_KV_HEADS
N_MLA_LAYERS = (DEPTH + 1) // 2
N_GQA_LAYERS = DEPTH // 2

kernel_name = "hybrid_mla_axial_gqa_macaron_encoder"


def rms_norm(x, g):
    xf = x.astype(jnp.float32)
    y = xf * lax.rsqrt(jnp.mean(xf * xf, axis=-1, keepdims=True) + NORM_EPS)
    return (y * g.astype(jnp.float32)).astype(x.dtype)


def rope_angles(pos, dim):
    inv = ROPE_THETA ** (-jnp.arange(0, dim, 2, dtype=jnp.float32) / dim)
    return pos.astype(jnp.float32)[:, None] * inv[None, :]


def apply_rope(x, ang):
    xf = x.astype(jnp.float32)
    x1, x2 = jnp.split(xf, 2, axis=-1)
    c = jnp.cos(ang)[None, :, None, :]
    s = jnp.sin(ang)[None, :, None, :]
    return jnp.concatenate([x1 * c - x2 * s, x1 * s + x2 * c], axis=-1).astype(x.dtype)


def axial_rope(x, ang_row, ang_col):
    half = x.shape[-1] // 2
    return jnp.concatenate([apply_rope(x[..., :half], ang_row), apply_rope(x[..., half:], ang_col)], axis=-1)


def token_angles(n):
    rows_count = n // GRID_W
    pos1d = jnp.arange(N_META + n, dtype=jnp.float32)
    row = jnp.concatenate([jnp.full((N_META,), -1.0, dtype=jnp.float32),
                           jnp.repeat(jnp.arange(rows_count, dtype=jnp.float32), GRID_W)])
    col = jnp.concatenate([jnp.arange(N_META, dtype=jnp.float32),
                           jnp.tile(jnp.arange(GRID_W, dtype=jnp.float32), rows_count)])
    half = GQA_HEAD_DIM // 2
    return rope_angles(pos1d, MLA_ROPE), rope_angles(row, half), rope_angles(col, half)


def swiglu(x, wg, wu, wd):
    return (jax.nn.silu(x @ wg) * (x @ wu)) @ wd


def attend_block(qb, k, v, scale):
    s = jnp.einsum('bqhgd,bkhd->bhgqk', qb, k, preferred_element_type=jnp.float32) * scale
    p = jax.nn.softmax(s, axis=-1).astype(v.dtype)
    return jnp.einsum('bhgqk,bkhd->bqhgd', p, v)


def bidir_attention(q, k, v, scale):
    B, L, Hk, G, dq = q.shape
    n = L - N_META
    nb = n // Q_BLOCK
    dv = v.shape[-1]
    out_meta = attend_block(q[:, :N_META], k, v, scale)
    qr = q[:, N_META:].reshape(B, nb, Q_BLOCK, Hk, G, dq).transpose(1, 0, 2, 3, 4, 5)
    out_real = lax.map(lambda qb: attend_block(qb, k, v, scale), qr)
    out_real = out_real.transpose(1, 0, 2, 3, 4, 5).reshape(B, n, Hk, G, dv)
    return jnp.concatenate([out_meta, out_real], axis=1)


def mla_mixer(h, wq_a, q_norm, wq_b, wkv_a, kv_norm, wkv_b, wo, ang):
    B, L, _ = h.shape
    cq = rms_norm(h @ wq_a, q_norm)
    q = (cq @ wq_b).reshape(B, L, MLA_HEADS, MLA_NOPE + MLA_ROPE)
    q = jnp.concatenate([q[..., :MLA_NOPE], apply_rope(q[..., MLA_NOPE:], ang)], axis=-1)
    kv_a = h @ wkv_a
    c_kv = rms_norm(kv_a[..., :MLA_KV_LORA], kv_norm)
    k_pe = apply_rope(kv_a[..., None, MLA_KV_LORA:], ang)
    kv = (c_kv @ wkv_b).reshape(B, L, MLA_HEADS, MLA_NOPE + MLA_V)
    k = jnp.concatenate([kv[..., :MLA_NOPE],
                         jnp.broadcast_to(k_pe, (B, L, MLA_HEADS, MLA_ROPE))], axis=-1)
    v = kv[..., MLA_NOPE:]
    o = bidir_attention(q[:, :, :, None, :], k, v, (MLA_NOPE + MLA_ROPE) ** -0.5)
    return o.reshape(B, L, MLA_HEADS * MLA_V) @ wo


def gqa_mixer(h, wqkv, q_norm, k_norm, wo, ang_row, ang_col):
    B, L, _ = h.shape
    hd = GQA_HEAD_DIM
    qkv = h @ wqkv
    q, k, v = jnp.split(qkv, [GQA_Q_HEADS * hd, (GQA_Q_HEADS + GQA_KV_HEADS) * hd], axis=-1)
    q = rms_norm(q.reshape(B, L, GQA_Q_HEADS, hd), q_norm)
    k = rms_norm(k.reshape(B, L, GQA_KV_HEADS, hd), k_norm)
    v = v.reshape(B, L, GQA_KV_HEADS, hd)
    q = axial_rope(q, ang_row, ang_col).reshape(B, L, GQA_KV_HEADS, GQA_GROUP, hd)
    k = axial_rope(k, ang_row, ang_col)
    o = bidir_attention(q, k, v, hd ** -0.5)
    return o.reshape(B, L, GQA_Q_HEADS * hd) @ wo


def encoder_trunk(x, p):
    B, n, _ = x.shape
    ang_1d, ang_row, ang_col = token_angles(n)
    meta = jnp.broadcast_to(p['meta_tokens'].astype(x.dtype)[None], (B, N_META, D_MODEL))
    h = jnp.concatenate([meta, x], axis=1)
    for i in range(DEPTH):
        h = h + FFN_HALF * swiglu(rms_norm(h, p['norm_ffn1'][i]), p['ffn1_w_gate'][i],
                                  p['ffn1_w_up'][i], p['ffn1_w_down'][i])
        hn = rms_norm(h, p['norm_mix'][i])
        j = i // 2
        if i % 2 == 0:
            h = h + mla_mixer(hn, p['mla_wq_a'][j], p['mla_q_norm'][j], p['mla_wq_b'][j],
                              p['mla_wkv_a'][j], p['mla_kv_norm'][j], p['mla_wkv_b'][j],
                              p['mla_wo'][j], ang_1d)
        else:
            h = h + gqa_mixer(hn, p['gqa_wqkv'][j], p['gqa_q_norm'][j], p['gqa_k_norm'][j],
                              p['gqa_wo'][j], ang_row, ang_col)
        h = h + FFN_HALF * swiglu(rms_norm(h, p['norm_ffn2'][i]), p['ffn2_w_gate'][i],
                                  p['ffn2_w_up'][i], p['ffn2_w_down'][i])
    return rms_norm(h, p['final_norm'])[:, N_META:]


def _w(k, shape, fan_in):
    return jax.random.normal(k, shape, jnp.float32) * (fan_in ** -0.5)


def _g(k, shape):
    return 1.0 + 0.01 * jax.random.normal(k, shape, jnp.float32)


def setup_inputs(seed: int = 0) -> dict:
    key = jax.random.key(seed)
    ks = jax.random.split(key, 32)
    D, F, NA, NB = D_MODEL, D_FF, N_MLA_LAYERS, N_GQA_LAYERS
    hd = GQA_HEAD_DIM
    return {
        "x_prompt": jax.random.normal(ks[0], (BATCH, SEQ, D), jnp.float32),
        "x_sample": jax.random.normal(ks[1], (DEC_BATCH, DEC_SEQ, D), jnp.float32),
        "meta_tokens": jax.random.normal(ks[2], (N_META, D), jnp.float32),
        "norm_ffn1": _g(ks[3], (DEPTH, D)),
        "ffn1_w_gate": _w(ks[4], (DEPTH, D, F), D),
        "ffn1_w_up": _w(ks[5], (DEPTH, D, F), D),
        "ffn1_w_down": _w(ks[6], (DEPTH, F, D), F),
        "norm_mix": _g(ks[7], (DEPTH, D)),
        "norm_ffn2": _g(ks[8], (DEPTH, D)),
        "ffn2_w_gate": _w(ks[9], (DEPTH, D, F), D),
        "ffn2_w_up": _w(ks[10], (DEPTH, D, F), D),
        "ffn2_w_down": _w(ks[11], (DEPTH, F, D), F),
        "mla_wq_a": _w(ks[12], (NA, D, MLA_Q_LORA), D),
        "mla_q_norm": _g(ks[13], (NA, MLA_Q_LORA)),
        "mla_wq_b": _w(ks[14], (NA, MLA_Q_LORA, MLA_HEADS * (MLA_NOPE + MLA_ROPE)), MLA_Q_LORA),
        "mla_wkv_a": _w(ks[15], (NA, D, MLA_KV_LORA + MLA_ROPE), D),
        "mla_kv_norm": _g(ks[16], (NA, MLA_KV_LORA)),
        "mla_wkv_b": _w(ks[17], (NA, MLA_KV_LORA, MLA_HEADS * (MLA_NOPE + MLA_V)), MLA_KV_LORA),
        "mla_wo": _w(ks[18], (NA, MLA_HEADS * MLA_V, D), MLA_HEADS * MLA_V),
        "gqa_wqkv": _w(ks[19], (NB, D, (GQA_Q_HEADS + 2 * GQA_KV_HEADS) * hd), D),
        "gqa_q_norm": _g(ks[20], (NB, hd)),
        "gqa_k_norm": _g(ks[21], (NB, hd)),
        "gqa_wo": _w(ks[22], (NB, GQA_Q_HEADS * hd, D), GQA_Q_HEADS * hd),
        "final_norm": _g(ks[23], (D,)),
    }


def reference(x_prompt, x_sample, meta_tokens, norm_ffn1, ffn1_w_gate, ffn1_w_up, ffn1_w_down,
              norm_mix, norm_ffn2, ffn2_w_gate, ffn2_w_up, ffn2_w_down,
              mla_wq_a, mla_q_norm, mla_wq_b, mla_wkv_a, mla_kv_norm, mla_wkv_b, mla_wo,
              gqa_wqkv, gqa_q_norm, gqa_k_norm, gqa_wo, final_norm):
    p = dict(meta_tokens=meta_tokens, norm_ffn1=norm_ffn1, ffn1_w_gate=ffn1_w_gate,
             ffn1_w_up=ffn1_w_up, ffn1_w_down=ffn1_w_down, norm_mix=norm_mix,
             norm_ffn2=norm_ffn2, ffn2_w_gate=ffn2_w_gate, ffn2_w_up=ffn2_w_up,
             ffn2_w_down=ffn2_w_down, mla_wq_a=mla_wq_a, mla_q_norm=mla_q_norm,
             mla_wq_b=mla_wq_b, mla_wkv_a=mla_wkv_a, mla_kv_norm=mla_kv_norm,
             mla_wkv_b=mla_wkv_b, mla_wo=mla_wo, gqa_wqkv=gqa_wqkv, gqa_q_norm=gqa_q_norm,
             gqa_k_norm=gqa_k_norm, gqa_wo=gqa_wo, final_norm=final_norm)
    y_prompt = encoder_trunk(x_prompt, p)
    y_sample = encoder_trunk(x_sample, p)
    return (y_prompt, y_sample)
```

```python
import functools
import math
from typing import NamedTuple

import jax
import jax.numpy as jnp
from jax import lax
from jax.experimental import pallas as pl
from jax.experimental.pallas import tpu as pltpu

LANE = 128
BF16_SUBLANES = 16
MXU_DIM = 256
VMEM_BUDGET = 56 * 1024 * 1024

F32 = jnp.float32
BF16 = jnp.bfloat16
LOG2E = math.log2(math.e)


class Config(NamedTuple):
    d_model: int = 4096
    d_ff: int = 11008
    depth: int = 2
    n_meta: int = 16
    grid_w: int = 64
    eps: float = 1e-6
    theta: float = 10000.0
    mla_heads: int = 32
    mla_q_lora: int = 1024
    mla_kv_lora: int = 512
    mla_nope: int = 128
    mla_rope: int = 64
    mla_v: int = 128
    gqa_q_heads: int = 32
    gqa_kv_heads: int = 8
    gqa_head_dim: int = 128
    row_tile: int = 1040
    norm_rows: int = 480
    ff_tile: int = 512
    down_k: int = 2816
    out_cols: int = 1024
    mla_q_rows: int = 1040
    gqa_q_rows: int = 208
    key_chunk: int = 512


def _divisor_tile(total, target, mult):
    best = None
    for t in range(mult, min(total, target) + 1, mult):
        if total % t == 0:
            best = t
    assert best is not None, (total, target, mult)
    return best


def _round_up(x, m):
    return (x + m - 1) // m * m


def _params(block_bytes, semantics):
    limit = min(VMEM_BUDGET, max(16 * 1024 * 1024, int(block_bytes * 1.25) + (4 << 20)))
    return pltpu.CompilerParams(dimension_semantics=semantics, vmem_limit_bytes=limit)


def _rmsnorm_kernel(x_ref, g_ref, o_ref, *, eps):
    x = x_ref[...]
    ms = jnp.mean(x * x, axis=-1, keepdims=True)
    o_ref[...] = (x * lax.rsqrt(ms + eps) * g_ref[...]).astype(o_ref.dtype)


def _rmsnorm(x, g, cfg, out_dtype=BF16):
    rows, d = x.shape
    tm = _divisor_tile(rows, cfg.norm_rows, BF16_SUBLANES)
    nbytes = 2 * tm * d * (4 + jnp.dtype(out_dtype).itemsize)
    return pl.pallas_call(
        functools.partial(_rmsnorm_kernel, eps=cfg.eps),
        grid=(rows // tm,),
        in_specs=[pl.BlockSpec((tm, d), lambda i: (i, 0)),
                  pl.BlockSpec((1, d), lambda i: (0, 0))],
        out_specs=pl.BlockSpec((tm, d), lambda i: (i, 0)),
        out_shape=jax.ShapeDtypeStruct((rows, d), out_dtype),
        compiler_params=_params(nbytes, ("parallel",)),
        name="rmsnorm",
    )(x, g.reshape(1, d).astype(F32))


def _final_norm(h3, g, cfg, seq_lo, seq_n, n_real):
    _, _, d = h3.shape
    tr = _divisor_tile(n_real, cfg.norm_rows, 8)
    nbytes = 2 * tr * d * 8
    return pl.pallas_call(
        functools.partial(_rmsnorm_kernel, eps=cfg.eps),
        grid=(seq_n, n_real // tr),
        in_specs=[pl.BlockSpec((None, tr, d), lambda b, i: (b + seq_lo, i, 0)),
                  pl.BlockSpec((1, d), lambda b, i: (0, 0))],
        out_specs=pl.BlockSpec((None, tr, d), lambda b, i: (b, i, 0)),
        out_shape=jax.ShapeDtypeStruct((seq_n, n_real, d), F32),
        compiler_params=_params(nbytes, ("parallel", "parallel")),
        name="final_norm",
    )(h3, g.reshape(1, d).astype(F32))


def _gateup_kernel(x_ref, wg_ref, wu_ref, o_ref):
    x = x_ref[...]
    g = jnp.dot(x, wg_ref[...], preferred_element_type=F32)
    u = jnp.dot(x, wu_ref[...], preferred_element_type=F32)
    o_ref[...] = (g * jax.nn.sigmoid(g) * u).astype(o_ref.dtype)


def _gateup(x, wg, wu, cfg):
    rows, d = x.shape
    fp = wg.shape[1]
    tm = _divisor_tile(rows, cfg.row_tile, BF16_SUBLANES)
    tn = _divisor_tile(fp, cfg.ff_tile, LANE)
    nbytes = 2 * (tm * d * 2 + 2 * d * tn * 2 + tm * tn * 2) + 3 * tm * tn * 4
    return pl.pallas_call(
        _gateup_kernel,
        grid=(rows // tm, fp // tn),
        in_specs=[pl.BlockSpec((tm, d), lambda i, j: (i, 0)),
                  pl.BlockSpec((d, tn), lambda i, j: (0, j)),
                  pl.BlockSpec((d, tn), lambda i, j: (0, j))],
        out_specs=pl.BlockSpec((tm, tn), lambda i, j: (i, j)),
        out_shape=jax.ShapeDtypeStruct((rows, fp), BF16),
        compiler_params=_params(nbytes, ("parallel", "arbitrary")),
        name="ffn_gateup",
    )(x, wg, wu)


def _mm_res_kernel(a_ref, w_ref, r_ref, o_ref, *, scale, k_steps):
    part = jnp.dot(a_ref[...], w_ref[...], preferred_element_type=F32)
    if scale != 1.0:
        part = scale * part
    if k_steps == 1:
        o_ref[...] = r_ref[...] + part
    else:
        k = pl.program_id(2)

        @pl.when(k == 0)
        def _():
            o_ref[...] = r_ref[...] + part

        @pl.when(k > 0)
        def _():
            o_ref[...] += part


def _mm_residual(a, w, res, scale, cfg, k_target, col_target):
    rows, kdim = a.shape
    n = w.shape[1]
    tm = _divisor_tile(rows, cfg.row_tile, BF16_SUBLANES)
    tn = _divisor_tile(n, col_target, LANE)
    tk = _divisor_tile(kdim, k_target, LANE)
    k_steps = kdim // tk
    nbytes = 2 * (tm * tk * 2 + tk * tn * 2 + 2 * tm * tn * 4) + tm * tn * 4
    return pl.pallas_call(
        functools.partial(_mm_res_kernel, scale=scale, k_steps=k_steps),
        grid=(rows // tm, n // tn, k_steps),
        in_specs=[pl.BlockSpec((tm, tk), lambda i, j, k: (i, k)),
                  pl.BlockSpec((tk, tn), lambda i, j, k: (k, j)),
                  pl.BlockSpec((tm, tn), lambda i, j, k: (i, j))],
        out_specs=pl.BlockSpec((tm, tn), lambda i, j, k: (i, j)),
        out_shape=jax.ShapeDtypeStruct((rows, n), F32),
        input_output_aliases={2: 0},
        compiler_params=_params(nbytes, ("parallel", "parallel", "arbitrary")),
        name="matmul_residual",
    )(a, w, res)


def _rope_partner(x):
    lane = lax.broadcasted_iota(jnp.int32, x.shape, 1)
    first_half = (lane & 32) == 0
    return jnp.where(first_half, pltpu.roll(x, LANE - 32, 1), pltpu.roll(x, 32, 1))


def _rope(x, cos, sin_signed):
    return x * cos + _rope_partner(x) * sin_signed


def _rms(x, eps):
    return x * lax.rsqrt(jnp.mean(x * x, axis=-1, keepdims=True) + eps)


def _mla_a_kernel(x_ref, w_ref, gq_ref, gkv_ref, cos_ref, sin_ref,
                  cq_ref, ckv_ref, kpe_ref, *, q_lora, kv_lora, eps):
    acc = jnp.dot(x_ref[...], w_ref[...], preferred_element_type=F32)
    cq_ref[...] = (_rms(acc[:, :q_lora], eps) * gq_ref[...]).astype(cq_ref.dtype)
    ckv_ref[...] = (_rms(acc[:, q_lora:q_lora + kv_lora], eps) * gkv_ref[...]).astype(ckv_ref.dtype)
    pe = acc[:, q_lora + kv_lora:]
    kpe_ref[...] = _rope(pe, cos_ref[...], sin_ref[...]).astype(kpe_ref.dtype)


def _mla_a(x, w_a, gq, gkv, cos, sin, cfg, seq_rows):
    rows, d = x.shape
    q_lora, kv_lora = cfg.mla_q_lora, cfg.mla_kv_lora
    n = w_a.shape[1]
    tm = _divisor_tile(seq_rows, 640, BF16_SUBLANES)
    per_seq = seq_rows // tm
    nbytes = 2 * (tm * d * 2 + d * n * 2 + tm * n * 2 + 2 * tm * LANE * 4) + tm * n * 4
    row = lambda i: (i, 0)
    tab = lambda i: (i % per_seq, 0)
    fixed = lambda i: (0, 0)
    return pl.pallas_call(
        functools.partial(_mla_a_kernel, q_lora=q_lora, kv_lora=kv_lora, eps=cfg.eps),
        grid=(rows // tm,),
        in_specs=[pl.BlockSpec((tm, d), row), pl.BlockSpec((d, n), fixed),
                  pl.BlockSpec((1, q_lora), fixed), pl.BlockSpec((1, kv_lora), fixed),
                  pl.BlockSpec((tm, LANE), tab), pl.BlockSpec((tm, LANE), tab)],
        out_specs=[pl.BlockSpec((tm, q_lora), row), pl.BlockSpec((tm, kv_lora), row),
                   pl.BlockSpec((tm, LANE), row)],
        out_shape=[jax.ShapeDtypeStruct((rows, q_lora), BF16),
                   jax.ShapeDtypeStruct((rows, kv_lora), BF16),
                   jax.ShapeDtypeStruct((rows, LANE), BF16)],
        compiler_params=_params(nbytes, ("parallel",)),
        name="mla_latents",
    )(x, w_a, gq.reshape(1, -1).astype(F32), gkv.reshape(1, -1).astype(F32), cos, sin)


def _proj_plain_kernel(x_ref, w_ref, o_ref):
    o_ref[...] = jnp.dot(x_ref[...], w_ref[...], preferred_element_type=F32).astype(o_ref.dtype)


def _proj_mla_q_kernel(x_ref, w_ref, cos_ref, sin_ref, o_ref, *, scale):
    acc = jnp.dot(x_ref[...], w_ref[...], preferred_element_type=F32)
    cos, sin = cos_ref[...], sin_ref[...]
    for h in range(acc.shape[1] // (2 * LANE)):
        c0 = 2 * LANE * h
        o_ref[:, c0:c0 + LANE] = (acc[:, c0:c0 + LANE] * scale).astype(o_ref.dtype)
        rot = _rope(acc[:, c0 + LANE:c0 + 2 * LANE], cos, sin)
        o_ref[:, c0 + LANE:c0 + 2 * LANE] = (rot * scale).astype(o_ref.dtype)


def _proj_gqa_qk_kernel(x_ref, w_ref, gain_ref, cscale_ref, cos_ref, sin_ref, o_ref, *, eps):
    acc = jnp.dot(x_ref[...], w_ref[...], preferred_element_type=F32)
    cos, sin = cos_ref[...], sin_ref[...]
    for h in range(acc.shape[1] // LANE):
        sl = slice(LANE * h, LANE * (h + 1))
        y = _rms(acc[:, sl], eps) * gain_ref[:, sl]
        o_ref[:, sl] = (_rope(y, cos, sin) * cscale_ref[:, sl]).astype(o_ref.dtype)


def _project(x, w, cfg, seq_rows, col_tile, kernel_fn, row_vecs=(), tables=()):
    rows, kdim = x.shape
    n = w.shape[1]
    tm = _divisor_tile(seq_rows, cfg.row_tile, BF16_SUBLANES)
    per_seq = seq_rows // tm
    tn = _divisor_tile(n, col_tile, LANE)
    nbytes = (2 * (tm * kdim * 2 + kdim * tn * 2 + tm * tn * 2 + len(tables) * tm * LANE * 4)
              + 2 * tm * tn * 4)
    in_specs = [pl.BlockSpec((tm, kdim), lambda i, j: (i, 0)),
                pl.BlockSpec((kdim, tn), lambda i, j: (0, j))]
    in_specs += [pl.BlockSpec((1, tn), lambda i, j: (0, j)) for _ in row_vecs]
    in_specs += [pl.BlockSpec((tm, LANE), lambda i, j: (i % per_seq, 0)) for _ in tables]
    return pl.pallas_call(
        kernel_fn,
        grid=(rows // tm, n // tn),
        in_specs=in_specs,
        out_specs=pl.BlockSpec((tm, tn), lambda i, j: (i, j)),
        out_shape=jax.ShapeDtypeStruct((rows, n), BF16),
        compiler_params=_params(nbytes, ("parallel", "arbitrary")),
        name="projection",
    )(x, w, *row_vecs, *tables)


def _attn_kernel(*refs, n_k, groups, dq, dv, tk, n_real, n_tail, n_tail_valid):
    q_ref = refs[0]
    k_refs = refs[1:1 + n_k]
    v_ref = refs[1 + n_k]
    o_ref = refs[2 + n_k]
    m_ref, l_ref, acc_ref = refs[3 + n_k:]
    tq = q_ref.shape[0]

    if groups == 1:
        q = q_ref[...]
    else:
        q = jnp.concatenate([q_ref[:, g * dq:(g + 1) * dq] for g in range(groups)], axis=0)

    m_ref[...] = jnp.full(m_ref.shape, -jnp.inf, F32)
    l_ref[...] = jnp.zeros(l_ref.shape, F32)
    acc_ref[...] = jnp.zeros(acc_ref.shape, F32)

    def step(start, size, valid):
        if n_k == 1:
            k = k_refs[0][pl.ds(start, size), :]
        else:
            k = jnp.concatenate([r[pl.ds(start, size), :] for r in k_refs], axis=-1)
        v = v_ref[pl.ds(start, size), :]
        s = lax.dot_general(q, k, (((1,), (1,)), ((), ())), preferred_element_type=F32)
        if valid is not None:
            col = lax.broadcasted_iota(jnp.int32, s.shape, 1)
            s = jnp.where(col < valid, s, -jnp.inf)
        m_prev = m_ref[...]
        m_new = jnp.maximum(m_prev, jnp.max(s, axis=-1, keepdims=True))
        alpha = jnp.exp2(m_prev - m_new)
        p = jnp.exp2(s - m_new)
        l_ref[...] = alpha * l_ref[...] + jnp.sum(p, axis=-1, keepdims=True)
        acc_ref[...] = alpha * acc_ref[...] + jnp.dot(p.astype(v.dtype), v,
                                                      preferred_element_type=F32)
        m_ref[...] = m_new

    def body(c, carry):
        step(pl.multiple_of(c * tk, tk), tk, None)
        return carry

    lax.fori_loop(0, n_real // tk, body, 0)
    step(n_real, n_tail, n_tail_valid)

    out = acc_ref[...] / l_ref[...]
    for g in range(groups):
        o_ref[:, g * dv:(g + 1) * dv] = out[g * tq:(g + 1) * tq].astype(o_ref.dtype)


def _attention(q3, k_list, v3, v_col0, cfg, *, kv_heads, groups, dq, tq, n_real):
    nseq, lp, _ = q3.shape
    dv = LANE
    n_tail = lp - n_real
    tk = _divisor_tile(n_real, cfg.key_chunk, LANE)
    m_rows = groups * tq
    in_specs = [pl.BlockSpec((None, tq, groups * dq), lambda b, h, i: (b, i, h))]
    k_arrays = []
    for arr, col0, per_head in k_list:
        if per_head:
            idx = lambda b, h, i, col0=col0: (b, 0, col0 + h)
        else:
            idx = lambda b, h, i, col0=col0: (b, 0, col0)
        in_specs.append(pl.BlockSpec((None, lp, LANE), idx))
        k_arrays.append(arr)
    in_specs.append(pl.BlockSpec((None, lp, dv), lambda b, h, i: (b, 0, v_col0 + h)))
    nbytes = (2 * (tq * groups * dq * 2 + (len(k_list) + 1) * lp * LANE * 2 + tq * groups * dv * 2)
              + m_rows * (2 * LANE + dv) * 4 + 3 * m_rows * tk * 4)
    return pl.pallas_call(
        functools.partial(_attn_kernel, n_k=len(k_list), groups=groups, dq=dq, dv=dv, tk=tk,
                          n_real=n_real, n_tail=n_tail, n_tail_valid=cfg.n_meta),
        grid=(nseq, kv_heads, lp // tq),
        in_specs=in_specs,
        out_specs=pl.BlockSpec((None, tq, groups * dv), lambda b, h, i: (b, i, h)),
        out_shape=jax.ShapeDtypeStruct((nseq, lp, kv_heads * groups * dv), BF16),
        scratch_shapes=[pltpu.VMEM((m_rows, 1), F32), pltpu.VMEM((m_rows, 1), F32),
                        pltpu.VMEM((m_rows, dv), F32)],
        compiler_params=_params(nbytes, ("parallel", "parallel", "arbitrary")),
        name="attention",
    )(q3, *k_arrays, v3)


def _rope_tables(cfg, n_real, lp):
    n_meta = cfg.n_meta
    half = cfg.mla_rope // 2
    inv = cfg.theta ** (-jnp.arange(0, cfg.mla_rope, 2, dtype=F32) / cfg.mla_rope)
    pad = lp - n_real - n_meta
    tok = jnp.arange(n_real, dtype=jnp.int32)
    meta = jnp.arange(n_meta, dtype=F32)
    zeros = jnp.zeros((pad,), F32)
    pos1d = jnp.concatenate([(tok + n_meta).astype(F32), meta, zeros])
    row = jnp.concatenate([(tok // cfg.grid_w).astype(F32), jnp.full((n_meta,), -1.0, F32), zeros])
    col = jnp.concatenate([(tok % cfg.grid_w).astype(F32), meta, zeros])

    def cs(pos):
        ang = pos[:, None] * inv[None, :]
        c, s = jnp.cos(ang), jnp.sin(ang)
        return jnp.concatenate([c, c], axis=1), jnp.concatenate([-s, s], axis=1)

    c1, s1 = cs(pos1d)
    cr, sr = cs(row)
    cc, sc = cs(col)
    assert 4 * half == LANE
    mla = (jnp.concatenate([c1, c1], axis=1), jnp.concatenate([s1, s1], axis=1))
    gqa = (jnp.concatenate([cr, cc], axis=1), jnp.concatenate([sr, sc], axis=1))
    return mla, gqa


def _ffn(h, g, wg, wu, wd, cfg):
    d, f = wg.shape
    fp = _round_up(f, cfg.ff_tile)
    wg = jnp.pad(wg.astype(BF16), ((0, 0), (0, fp - f)))
    wu = jnp.pad(wu.astype(BF16), ((0, 0), (0, fp - f)))
    wd = jnp.pad(wd.astype(BF16), ((0, fp - f), (0, 0)))
    x = _rmsnorm(h, g, cfg)
    a = _gateup(x, wg, wu, cfg)
    return _mm_residual(a, wd, h, 0.5, cfg, cfg.down_k, cfg.out_cols)


def _mla_layer(h, p, j, tables, cfg, nseq, lp, n_real):
    d = cfg.d_model
    heads, nope, rope, dv = cfg.mla_heads, cfg.mla_nope, cfg.mla_rope, cfg.mla_v
    q_lora, kv_lora = cfg.mla_q_lora, cfg.mla_kv_lora
    assert nope == LANE and dv == LANE and 2 * rope == LANE
    cos, sin = tables
    x = _rmsnorm(h, p['norm_mix_i'], cfg)

    wkv_a = p['mla_wkv_a'][j]
    w_a = jnp.concatenate([p['mla_wq_a'][j], wkv_a, jnp.zeros((d, LANE - rope), F32)],
                          axis=1).astype(BF16)
    cq, ckv, kpe = _mla_a(x, w_a, p['mla_q_norm'][j], p['mla_kv_norm'][j], cos, sin, cfg, lp)

    wq_b = p['mla_wq_b'][j].reshape(q_lora, heads, nope + rope)
    wq_b = jnp.pad(wq_b, ((0, 0), (0, 0), (0, 2 * LANE - nope - rope)))
    wq_b = wq_b.reshape(q_lora, heads * 2 * LANE).astype(BF16)
    scale = (nope + rope) ** -0.5 * LOG2E
    q = _project(cq, wq_b, cfg, lp, 1024, functools.partial(_proj_mla_q_kernel, scale=scale),
                 tables=(cos, sin))

    wkv_b = p['mla_wkv_b'][j].reshape(kv_lora, heads, nope + dv)
    wkv_b = jnp.concatenate([wkv_b[:, :, :nope].reshape(kv_lora, heads * nope),
                             wkv_b[:, :, nope:].reshape(kv_lora, heads * dv)], axis=1).astype(BF16)
    kv = _project(ckv, wkv_b, cfg, lp, 4096, _proj_plain_kernel)

    tq = _divisor_tile(lp, cfg.mla_q_rows, BF16_SUBLANES)
    o = _attention(q.reshape(nseq, lp, -1),
                   [(kv.reshape(nseq, lp, -1), 0, True), (kpe.reshape(nseq, lp, LANE), 0, False)],
                   kv.reshape(nseq, lp, -1), heads, cfg,
                   kv_heads=heads, groups=1, dq=2 * LANE, tq=tq, n_real=n_real)
    return _mm_residual(o.reshape(nseq * lp, heads * dv), p['mla_wo'][j].astype(BF16), h, 1.0, cfg,
                        heads * dv, cfg.out_cols // 2)


def _gqa_layer(h, p, j, tables, cfg, nseq, lp, n_real):
    hq, hk, hd = cfg.gqa_q_heads, cfg.gqa_kv_heads, cfg.gqa_head_dim
    assert hd == LANE
    cos, sin = tables
    x = _rmsnorm(h, p['norm_mix_i'], cfg)
    wqkv = p['gqa_wqkv'][j].astype(BF16)
    n_qk = (hq + hk) * hd
    gain = jnp.concatenate([jnp.tile(p['gqa_q_norm'][j], hq), jnp.tile(p['gqa_k_norm'][j], hk)])
    cscale = jnp.concatenate([jnp.full((hq * hd,), hd ** -0.5 * LOG2E, F32),
                              jnp.ones((hk * hd,), F32)])
    qk = _project(x, wqkv[:, :n_qk], cfg, lp, 512,
                  functools.partial(_proj_gqa_qk_kernel, eps=cfg.eps),
                  row_vecs=(gain.reshape(1, -1).astype(F32), cscale.reshape(1, -1)),
                  tables=(cos, sin))
    v = _project(x, wqkv[:, n_qk:], cfg, lp, 1024, _proj_plain_kernel)

    groups = hq // hk
    tq = _divisor_tile(lp, cfg.gqa_q_rows, BF16_SUBLANES)
    qk3 = qk.reshape(nseq, lp, -1)
    o = _attention(qk3, [(qk3, hq, True)], v.reshape(nseq, lp, -1), 0, cfg,
                   kv_heads=hk, groups=groups, dq=hd, tq=tq, n_real=n_real)
    return _mm_residual(o.reshape(nseq * lp, hq * hd), p['gqa_wo'][j].astype(BF16), h, 1.0, cfg,
                        hq * hd, cfg.out_cols // 2)


def _forward(cfg, x_prompt, x_sample, p):
    n_real = x_prompt.shape[1]
    assert x_sample.shape[1] == n_real and cfg.n_meta <= LANE and n_real % cfg.grid_w == 0
    d = cfg.d_model
    n_prompt, n_sample = x_prompt.shape[0], x_sample.shape[0]
    nseq = n_prompt + n_sample
    lp = n_real + LANE
    x = jnp.concatenate([x_prompt, x_sample], axis=0)
    meta = jnp.broadcast_to(p['meta_tokens'].astype(F32)[None], (nseq, cfg.n_meta, d))
    h = jnp.concatenate([x, meta, jnp.zeros((nseq, lp - n_real - cfg.n_meta, d), F32)], axis=1)
    h = h.reshape(nseq * lp, d)
    mla_tab, gqa_tab = _rope_tables(cfg, n_real, lp)

    for i in range(cfg.depth):
        h = _ffn(h, p['norm_ffn1'][i], p['ffn1_w_gate'][i], p['ffn1_w_up'][i], p['ffn1_w_down'][i], cfg)
        pi = dict(p, norm_mix_i=p['norm_mix'][i])
        if i % 2 == 0:
            h = _mla_layer(h, pi, i // 2, mla_tab, cfg, nseq, lp, n_real)
        else:
            h = _gqa_layer(h, pi, i // 2, gqa_tab, cfg, nseq, lp, n_real)
        h = _ffn(h, p['norm_ffn2'][i], p['ffn2_w_gate'][i], p['ffn2_w_up'][i], p['ffn2_w_down'][i], cfg)

    h3 = h.reshape(nseq, lp, d)
    y_prompt = _final_norm(h3, p['final_norm'], cfg, 0, n_prompt, n_real)
    y_sample = _final_norm(h3, p['final_norm'], cfg, n_prompt, n_sample, n_real)
    return y_prompt, y_sample


def kernel(x_prompt, x_sample, meta_tokens, norm_ffn1, ffn1_w_gate, ffn1_w_up, ffn1_w_down,
           norm_mix, norm_ffn2, ffn2_w_gate, ffn2_w_up, ffn2_w_down,
           mla_wq_a, mla_q_norm, mla_wq_b, mla_wkv_a, mla_kv_norm, mla_wkv_b, mla_wo,
           gqa_wqkv, gqa_q_norm, gqa_k_norm, gqa_wo, final_norm):
    p = dict(meta_tokens=meta_tokens, norm_ffn1=norm_ffn1, ffn1_w_gate=ffn1_w_gate,
             ffn1_w_up=ffn1_w_up, ffn1_w_down=ffn1_w_down, norm_mix=norm_mix,
             norm_ffn2=norm_ffn2, ffn2_w_gate=ffn2_w_gate, ffn2_w_up=ffn2_w_up,
             ffn2_w_down=ffn2_w_down, mla_wq_a=mla_wq_a, mla_q_norm=mla_q_norm,
             mla_wq_b=mla_wq_b, mla_wkv_a=mla_wkv_a, mla_kv_norm=mla_kv_norm,
             mla_wkv_b=mla_wkv_b, mla_wo=mla_wo, gqa_wqkv=gqa_wqkv, gqa_q_norm=gqa_q_norm,
             gqa_k_norm=gqa_k_norm, gqa_wo=gqa_wo, final_norm=final_norm)
    return _forward(Config(), x_prompt, x_sample, p)
```

```python
import functools
import math
from typing import NamedTuple

import jax
import jax.numpy as jnp
from jax import lax
from jax.experimental import pallas as pl
from jax.experimental.pallas import tpu as pltpu

LANE = 128
BF16_SUBLANES = 16
VMEM_BUDGET = 56 * 1024 * 1024

F32 = jnp.float32
BF16 = jnp.bfloat16
LOG2E = math.log2(math.e)


class Config(NamedTuple):
    d_model: int = 4096
    d_ff: int = 11008
    depth: int = 2
    n_meta: int = 16
    grid_w: int = 64
    eps: float = 1e-6
    theta: float = 10000.0
    mla_heads: int = 32
    mla_q_lora: int = 1024
    mla_kv_lora: int = 512
    mla_nope: int = 128
    mla_rope: int = 64
    mla_v: int = 128
    gqa_q_heads: int = 32
    gqa_kv_heads: int = 8
    gqa_head_dim: int = 128
    row_tile: int = 1040
    lane_row_tile: int = 640
    norm_rows: int = 480
    ff_tile: int = 512
    down_k: int = 2816
    out_cols: int = 1024
    mla_q_rows: int = 512
    gqa_q_rows: int = 128
    key_chunk: int = 1024


def _divisor_tile(total, target, mult):
    best = None
    for t in range(mult, min(total, target) + 1, mult):
        if total % t == 0:
            best = t
    assert best is not None, (total, target, mult)
    return best


def _round_up(x, m):
    return (x + m - 1) // m * m


def _params(block_bytes, semantics):
    limit = min(VMEM_BUDGET, max(16 * 1024 * 1024, int(block_bytes * 1.25) + (4 << 20)))
    return pltpu.CompilerParams(dimension_semantics=semantics, vmem_limit_bytes=limit)


def _rmsnorm_kernel(x_ref, g_ref, o_ref, *, eps):
    x = x_ref[...]
    ms = jnp.mean(x * x, axis=-1, keepdims=True)
    o_ref[...] = (x * lax.rsqrt(ms + eps) * g_ref[...]).astype(o_ref.dtype)


def _rmsnorm(x, g, cfg, out_dtype=BF16):
    rows, d = x.shape
    tm = _divisor_tile(rows, cfg.norm_rows, BF16_SUBLANES)
    nbytes = 2 * tm * d * (4 + jnp.dtype(out_dtype).itemsize)
    return pl.pallas_call(
        functools.partial(_rmsnorm_kernel, eps=cfg.eps),
        grid=(rows // tm,),
        in_specs=[pl.BlockSpec((tm, d), lambda i: (i, 0)),
                  pl.BlockSpec((1, d), lambda i: (0, 0))],
        out_specs=pl.BlockSpec((tm, d), lambda i: (i, 0)),
        out_shape=jax.ShapeDtypeStruct((rows, d), out_dtype),
        compiler_params=_params(nbytes, ("parallel",)),
        name="rmsnorm",
    )(x, g.reshape(1, d).astype(F32))


def _final_norm(h3, g, cfg, seq_lo, seq_n, n_real):
    _, _, d = h3.shape
    tr = _divisor_tile(n_real, cfg.norm_rows, 8)
    nbytes = 2 * tr * d * 8
    return pl.pallas_call(
        functools.partial(_rmsnorm_kernel, eps=cfg.eps),
        grid=(seq_n, n_real // tr),
        in_specs=[pl.BlockSpec((None, tr, d), lambda b, i: (b + seq_lo, i, 0)),
                  pl.BlockSpec((1, d), lambda b, i: (0, 0))],
        out_specs=pl.BlockSpec((None, tr, d), lambda b, i: (b, i, 0)),
        out_shape=jax.ShapeDtypeStruct((seq_n, n_real, d), F32),
        compiler_params=_params(nbytes, ("parallel", "parallel")),
        name="final_norm",
    )(h3, g.reshape(1, d).astype(F32))


def _gateup_kernel(x_ref, wg_ref, wu_ref, o_ref):
    x = x_ref[...]
    g = jnp.dot(x, wg_ref[...], preferred_element_type=F32)
    u = jnp.dot(x, wu_ref[...], preferred_element_type=F32)
    o_ref[...] = (g * jax.nn.sigmoid(g) * u).astype(o_ref.dtype)


def _gateup(x, wg, wu, cfg):
    rows, d = x.shape
    fp = wg.shape[1]
    tm = _divisor_tile(rows, cfg.row_tile, BF16_SUBLANES)
    tn = _divisor_tile(fp, cfg.ff_tile, LANE)
    nbytes = 2 * (tm * d * 2 + 2 * d * tn * 2 + tm * tn * 2) + 3 * tm * tn * 4
    return pl.pallas_call(
        _gateup_kernel,
        grid=(rows // tm, fp // tn),
        in_specs=[pl.BlockSpec((tm, d), lambda i, j: (i, 0)),
                  pl.BlockSpec((d, tn), lambda i, j: (0, j)),
                  pl.BlockSpec((d, tn), lambda i, j: (0, j))],
        out_specs=pl.BlockSpec((tm, tn), lambda i, j: (i, j)),
        out_shape=jax.ShapeDtypeStruct((rows, fp), BF16),
        compiler_params=_params(nbytes, ("parallel", "arbitrary")),
        name="ffn_gateup",
    )(x, wg, wu)


def _mm_res_kernel(a_ref, w_ref, r_ref, o_ref, *, scale, k_steps):
    part = jnp.dot(a_ref[...], w_ref[...], preferred_element_type=F32)
    if scale != 1.0:
        part = scale * part
    if k_steps == 1:
        o_ref[...] = r_ref[...] + part
    else:
        k = pl.program_id(2)

        @pl.when(k == 0)
        def _():
            o_ref[...] = r_ref[...] + part

        @pl.when(k > 0)
        def _():
            o_ref[...] += part


def _mm_residual(a, w, res, scale, cfg, k_target, col_target):
    rows, kdim = a.shape
    n = w.shape[1]
    tm = _divisor_tile(rows, cfg.row_tile, BF16_SUBLANES)
    tn = _divisor_tile(n, col_target, LANE)
    tk = _divisor_tile(kdim, k_target, LANE)
    k_steps = kdim // tk
    nbytes = 2 * (tm * tk * 2 + tk * tn * 2 + 2 * tm * tn * 4) + tm * tn * 4
    return pl.pallas_call(
        functools.partial(_mm_res_kernel, scale=scale, k_steps=k_steps),
        grid=(rows // tm, n // tn, k_steps),
        in_specs=[pl.BlockSpec((tm, tk), lambda i, j, k: (i, k)),
                  pl.BlockSpec((tk, tn), lambda i, j, k: (k, j)),
                  pl.BlockSpec((tm, tn), lambda i, j, k: (i, j))],
        out_specs=pl.BlockSpec((tm, tn), lambda i, j, k: (i, j)),
        out_shape=jax.ShapeDtypeStruct((rows, n), F32),
        input_output_aliases={2: 0},
        compiler_params=_params(nbytes, ("parallel", "parallel", "arbitrary")),
        name="matmul_residual",
    )(a, w, res)


def _rope_partner(x):
    lane = lax.broadcasted_iota(jnp.int32, x.shape, 1)
    first_half = (lane & 32) == 0
    return jnp.where(first_half, pltpu.roll(x, LANE - 32, 1), pltpu.roll(x, 32, 1))


def _rope(x, cos, sin_signed):
    return x * cos + _rope_partner(x) * sin_signed


def _rms(x, eps):
    return x * lax.rsqrt(jnp.mean(x * x, axis=-1, keepdims=True) + eps)


def _mla_a_kernel(x_ref, w_ref, gq_ref, gkv_ref, cos_ref, sin_ref,
                  cq_ref, ckv_ref, kpe_ref, *, q_lora, kv_lora, eps):
    acc = jnp.dot(x_ref[...], w_ref[...], preferred_element_type=F32)
    cq_ref[...] = (_rms(acc[:, :q_lora], eps) * gq_ref[...]).astype(cq_ref.dtype)
    ckv_ref[...] = (_rms(acc[:, q_lora:q_lora + kv_lora], eps) * gkv_ref[...]).astype(ckv_ref.dtype)
    pe = acc[:, q_lora + kv_lora:]
    kpe_ref[...] = _rope(pe, cos_ref[...], sin_ref[...]).astype(kpe_ref.dtype)


def _mla_a(x, w_a, gq, gkv, cos, sin, cfg, seq_rows):
    rows, d = x.shape
    q_lora, kv_lora = cfg.mla_q_lora, cfg.mla_kv_lora
    n = w_a.shape[1]
    tm = _divisor_tile(seq_rows, cfg.lane_row_tile, BF16_SUBLANES)
    per_seq = seq_rows // tm
    nbytes = 2 * (tm * d * 2 + d * n * 2 + tm * n * 2 + 2 * tm * LANE * 4) + tm * n * 4
    row = lambda i: (i, 0)
    tab = lambda i: (i % per_seq, 0)
    fixed = lambda i: (0, 0)
    return pl.pallas_call(
        functools.partial(_mla_a_kernel, q_lora=q_lora, kv_lora=kv_lora, eps=cfg.eps),
        grid=(rows // tm,),
        in_specs=[pl.BlockSpec((tm, d), row), pl.BlockSpec((d, n), fixed),
                  pl.BlockSpec((1, q_lora), fixed), pl.BlockSpec((1, kv_lora), fixed),
                  pl.BlockSpec((tm, LANE), tab), pl.BlockSpec((tm, LANE), tab)],
        out_specs=[pl.BlockSpec((tm, q_lora), row), pl.BlockSpec((tm, kv_lora), row),
                   pl.BlockSpec((tm, LANE), row)],
        out_shape=[jax.ShapeDtypeStruct((rows, q_lora), BF16),
                   jax.ShapeDtypeStruct((rows, kv_lora), BF16),
                   jax.ShapeDtypeStruct((rows, LANE), BF16)],
        compiler_params=_params(nbytes, ("parallel",)),
        name="mla_latents",
    )(x, w_a, gq.reshape(1, -1).astype(F32), gkv.reshape(1, -1).astype(F32), cos, sin)


def _proj_plain_kernel(x_ref, w_ref, o_ref):
    o_ref[...] = jnp.dot(x_ref[...], w_ref[...], preferred_element_type=F32).astype(o_ref.dtype)


def _proj_mla_q_kernel(x_ref, w_ref, cos_ref, sin_ref, o_ref, *, scale):
    acc = jnp.dot(x_ref[...], w_ref[...], preferred_element_type=F32)
    cos, sin = cos_ref[...], sin_ref[...]
    for h in range(acc.shape[1] // (2 * LANE)):
        c0 = 2 * LANE * h
        o_ref[:, c0:c0 + LANE] = (acc[:, c0:c0 + LANE] * scale).astype(o_ref.dtype)
        rot = _rope(acc[:, c0 + LANE:c0 + 2 * LANE], cos, sin)
        o_ref[:, c0 + LANE:c0 + 2 * LANE] = (rot * scale).astype(o_ref.dtype)


def _proj_gqa_qk_kernel(x_ref, w_ref, gain_ref, cscale_ref, cos_ref, sin_ref, o_ref, *, eps):
    acc = jnp.dot(x_ref[...], w_ref[...], preferred_element_type=F32)
    cos, sin = cos_ref[...], sin_ref[...]
    for h in range(acc.shape[1] // LANE):
        sl = slice(LANE * h, LANE * (h + 1))
        y = _rms(acc[:, sl], eps) * gain_ref[:, sl]
        o_ref[:, sl] = (_rope(y, cos, sin) * cscale_ref[:, sl]).astype(o_ref.dtype)


def _project(x, w, cfg, seq_rows, col_tile, kernel_fn, row_vecs=(), tables=()):
    rows, kdim = x.shape
    n = w.shape[1]
    tm = _divisor_tile(seq_rows, cfg.row_tile, BF16_SUBLANES)
    per_seq = seq_rows // tm
    tn = _divisor_tile(n, col_tile, LANE)
    nbytes = (2 * (tm * kdim * 2 + kdim * tn * 2 + tm * tn * 2 + len(tables) * tm * LANE * 4)
              + 2 * tm * tn * 4)
    in_specs = [pl.BlockSpec((tm, kdim), lambda i, j: (i, 0)),
                pl.BlockSpec((kdim, tn), lambda i, j: (0, j))]
    in_specs += [pl.BlockSpec((1, tn), lambda i, j: (0, j)) for _ in row_vecs]
    in_specs += [pl.BlockSpec((tm, LANE), lambda i, j: (i % per_seq, 0)) for _ in tables]
    return pl.pallas_call(
        kernel_fn,
        grid=(rows // tm, n // tn),
        in_specs=in_specs,
        out_specs=pl.BlockSpec((tm, tn), lambda i, j: (i, j)),
        out_shape=jax.ShapeDtypeStruct((rows, n), BF16),
        compiler_params=_params(nbytes, ("parallel", "arbitrary")),
        name="projection",
    )(x, w, *row_vecs, *tables)


def _proj_t_kernel(wt_ref, x_ref, o_ref):
    o_ref[...] = lax.dot_general(wt_ref[...], x_ref[...], (((1,), (1,)), ((), ())),
                                 preferred_element_type=F32).astype(o_ref.dtype)


def _project_t(x, wt, cfg, seq_rows, n_tile):
    rows, kdim = x.shape
    n = wt.shape[0]
    tm = _divisor_tile(seq_rows, cfg.lane_row_tile, LANE)
    tn = _divisor_tile(n, n_tile, BF16_SUBLANES)
    nbytes = 2 * (tm * kdim * 2 + kdim * tn * 2 + tm * tn * 2) + tm * tn * 4
    return pl.pallas_call(
        _proj_t_kernel,
        grid=(rows // tm, n // tn),
        in_specs=[pl.BlockSpec((tn, kdim), lambda i, j: (j, 0)),
                  pl.BlockSpec((tm, kdim), lambda i, j: (i, 0))],
        out_specs=pl.BlockSpec((tn, tm), lambda i, j: (j, i)),
        out_shape=jax.ShapeDtypeStruct((n, rows), BF16),
        compiler_params=_params(nbytes, ("parallel", "arbitrary")),
        name="projection_t",
    )(wt, x)


def _attn_kernel(*refs, n_k, groups, dq, dv, tk, n_real, n_tail, n_tail_valid):
    q_ref = refs[0]
    k_refs = refs[1:1 + n_k]
    vt_ref = refs[1 + n_k]
    o_ref = refs[2 + n_k]
    m_ref, l_ref, acc_ref = refs[3 + n_k:]
    tq = q_ref.shape[0]

    if groups == 1:
        q = q_ref[...]
    else:
        q = jnp.concatenate([q_ref[:, g * dq:(g + 1) * dq] for g in range(groups)], axis=0)

    m_ref[...] = jnp.full(m_ref.shape, -jnp.inf, F32)
    l_ref[...] = jnp.zeros(l_ref.shape, F32)
    acc_ref[...] = jnp.zeros(acc_ref.shape, F32)

    def scores(start, size, valid):
        if n_k == 1:
            k = k_refs[0][start:start + size, :]
        else:
            k = jnp.concatenate([r[start:start + size, :] for r in k_refs], axis=-1)
        s = lax.dot_general(k, q, (((1,), (1,)), ((), ())), preferred_element_type=F32)
        if valid is not None:
            row = lax.broadcasted_iota(jnp.int32, s.shape, 0)
            s = jnp.where(row < valid, s, -jnp.inf)
        return s

    def update(s, start, size):
        vt = vt_ref[:, start:start + size]
        m_prev = m_ref[...]
        m_new = jnp.maximum(m_prev, jnp.max(s, axis=0, keepdims=True))
        alpha = jnp.exp2(m_prev - m_new)
        p = jnp.exp2(s - m_new)
        l_ref[...] = alpha * l_ref[...] + jnp.sum(p, axis=0, keepdims=True)
        acc_ref[...] = alpha * acc_ref[...] + jnp.dot(vt, p.astype(vt.dtype),
                                                      preferred_element_type=F32)
        m_ref[...] = m_new

    chunks = [(c * tk, tk, None) for c in range(n_real // tk)] + [(n_real, n_tail, n_tail_valid)]
    s = scores(*chunks[0])
    for c, (start, size, _) in enumerate(chunks):
        s_next = scores(*chunks[c + 1]) if c + 1 < len(chunks) else None
        update(s, start, size)
        s = s_next

    out = (acc_ref[...] / l_ref[...]).T
    for g in range(groups):
        o_ref[:, g * dv:(g + 1) * dv] = out[g * tq:(g + 1) * tq].astype(o_ref.dtype)


def _attention(q3, k_list, vt, cfg, *, kv_heads, groups, dq, tq, n_real):
    nseq, lp, _ = q3.shape
    dv = LANE
    n_tail = lp - n_real
    tk = _divisor_tile(n_real, cfg.key_chunk, LANE)
    n_lanes = groups * tq
    in_specs = [pl.BlockSpec((None, tq, groups * dq), lambda b, h, i: (b, i, h))]
    k_arrays = []
    for arr, col0, per_head in k_list:
        if per_head:
            idx = lambda b, h, i, col0=col0: (b, 0, col0 + h)
        else:
            idx = lambda b, h, i, col0=col0: (b, 0, col0)
        in_specs.append(pl.BlockSpec((None, lp, LANE), idx))
        k_arrays.append(arr)
    in_specs.append(pl.BlockSpec((dv, lp), lambda b, h, i: (h, b)))
    nbytes = (2 * (tq * groups * dq * 2 + (len(k_list) + 1) * lp * LANE * 2 + tq * groups * dv * 2)
              + (dv + 16) * n_lanes * 4 + 4 * tk * n_lanes * 4)
    return pl.pallas_call(
        functools.partial(_attn_kernel, n_k=len(k_list), groups=groups, dq=dq, dv=dv, tk=tk,
                          n_real=n_real, n_tail=n_tail, n_tail_valid=cfg.n_meta),
        grid=(nseq, kv_heads, pl.cdiv(lp, tq)),
        in_specs=in_specs,
        out_specs=pl.BlockSpec((None, tq, groups * dv), lambda b, h, i: (b, i, h)),
        out_shape=jax.ShapeDtypeStruct((nseq, lp, kv_heads * groups * dv), BF16),
        scratch_shapes=[pltpu.VMEM((1, n_lanes), F32), pltpu.VMEM((1, n_lanes), F32),
                        pltpu.VMEM((dv, n_lanes), F32)],
        compiler_params=_params(nbytes, ("parallel", "parallel", "arbitrary")),
        name="attention",
    )(q3, *k_arrays, vt)


def _rope_tables(cfg, n_real, lp):
    n_meta = cfg.n_meta
    half = cfg.mla_rope // 2
    inv = cfg.theta ** (-jnp.arange(0, cfg.mla_rope, 2, dtype=F32) / cfg.mla_rope)
    pad = lp - n_real - n_meta
    tok = jnp.arange(n_real, dtype=jnp.int32)
    meta = jnp.arange(n_meta, dtype=F32)
    zeros = jnp.zeros((pad,), F32)
    pos1d = jnp.concatenate([(tok + n_meta).astype(F32), meta, zeros])
    row = jnp.concatenate([(tok // cfg.grid_w).astype(F32), jnp.full((n_meta,), -1.0, F32), zeros])
    col = jnp.concatenate([(tok % cfg.grid_w).astype(F32), meta, zeros])

    def cs(pos):
        ang = pos[:, None] * inv[None, :]
        c, s = jnp.cos(ang), jnp.sin(ang)
        return jnp.concatenate([c, c], axis=1), jnp.concatenate([-s, s], axis=1)

    c1, s1 = cs(pos1d)
    cr, sr = cs(row)
    cc, sc = cs(col)
    assert 4 * half == LANE
    mla = (jnp.concatenate([c1, c1], axis=1), jnp.concatenate([s1, s1], axis=1))
    gqa = (jnp.concatenate([cr, cc], axis=1), jnp.concatenate([sr, sc], axis=1))
    return mla, gqa


def _ffn_weights(wg, wu, wd, cfg):
    f = wg.shape[-1]
    fp = _round_up(f, cfg.ff_tile)
    wg = jnp.pad(wg.astype(BF16), ((0, 0), (0, 0), (0, fp - f)))
    wu = jnp.pad(wu.astype(BF16), ((0, 0), (0, 0), (0, fp - f)))
    wd = jnp.pad(wd.astype(BF16), ((0, 0), (0, fp - f), (0, 0)))
    return wg, wu, wd


def _ffn(h, g, wg, wu, wd, cfg):
    x = _rmsnorm(h, g, cfg)
    a = _gateup(x, wg, wu, cfg)
    return _mm_residual(a, wd, h, 0.5, cfg, cfg.down_k, cfg.out_cols)


def _mla_layer(h, p, j, tables, cfg, nseq, lp, n_real):
    d = cfg.d_model
    heads, nope, rope, dv = cfg.mla_heads, cfg.mla_nope, cfg.mla_rope, cfg.mla_v
    q_lora, kv_lora = cfg.mla_q_lora, cfg.mla_kv_lora
    assert nope == LANE and dv == LANE and 2 * rope == LANE
    cos, sin = tables
    x = _rmsnorm(h, p['norm_mix_i'], cfg)

    w_a = jnp.concatenate([p['mla_wq_a'][j], p['mla_wkv_a'][j], jnp.zeros((d, LANE - rope), F32)],
                          axis=1).astype(BF16)
    cq, ckv, kpe = _mla_a(x, w_a, p['mla_q_norm'][j], p['mla_kv_norm'][j], cos, sin, cfg, lp)

    wq_b = p['mla_wq_b'][j].reshape(q_lora, heads, nope + rope)
    wq_b = jnp.pad(wq_b, ((0, 0), (0, 0), (0, 2 * LANE - nope - rope)))
    wq_b = wq_b.reshape(q_lora, heads * 2 * LANE).astype(BF16)
    scale = (nope + rope) ** -0.5 * LOG2E
    q = _project(cq, wq_b, cfg, lp, 1024, functools.partial(_proj_mla_q_kernel, scale=scale),
                 tables=(cos, sin))

    wkv_b = p['mla_wkv_b'][j].reshape(kv_lora, heads, nope + dv)
    w_kn = wkv_b[:, :, :nope].reshape(kv_lora, heads * nope).astype(BF16)
    w_vt = wkv_b[:, :, nope:].reshape(kv_lora, heads * dv).T.astype(BF16)
    kn = _project(ckv, w_kn, cfg, lp, 4096, _proj_plain_kernel)
    vt = _project_t(ckv, w_vt, cfg, lp, 4096)

    o = _attention(q.reshape(nseq, lp, -1),
                   [(kn.reshape(nseq, lp, -1), 0, True), (kpe.reshape(nseq, lp, LANE), 0, False)],
                   vt, cfg, kv_heads=heads, groups=1, dq=2 * LANE, tq=cfg.mla_q_rows, n_real=n_real)
    return _mm_residual(o.reshape(nseq * lp, heads * dv), p['mla_wo'][j].astype(BF16), h, 1.0, cfg,
                        heads * dv, cfg.out_cols // 2)


def _gqa_layer(h, p, j, tables, cfg, nseq, lp, n_real):
    hq, hk, hd = cfg.gqa_q_heads, cfg.gqa_kv_heads, cfg.gqa_head_dim
    assert hd == LANE and hd // 2 == cfg.mla_rope
    cos, sin = tables
    x = _rmsnorm(h, p['norm_mix_i'], cfg)
    wqkv = p['gqa_wqkv'][j]
    n_qk = (hq + hk) * hd
    gain = jnp.concatenate([jnp.tile(p['gqa_q_norm'][j], hq), jnp.tile(p['gqa_k_norm'][j], hk)])
    cscale = jnp.concatenate([jnp.full((hq * hd,), hd ** -0.5 * LOG2E, F32),
                              jnp.ones((hk * hd,), F32)])
    qk = _project(x, wqkv[:, :n_qk].astype(BF16), cfg, lp, 512,
                  functools.partial(_proj_gqa_qk_kernel, eps=cfg.eps),
                  row_vecs=(gain.reshape(1, -1).astype(F32), cscale.reshape(1, -1)),
                  tables=(cos, sin))
    vt = _project_t(x, wqkv[:, n_qk:].T.astype(BF16), cfg, lp, 1024)

    qk3 = qk.reshape(nseq, lp, -1)
    o = _attention(qk3, [(qk3, hq, True)], vt, cfg, kv_heads=hk, groups=hq // hk, dq=hd,
                   tq=cfg.gqa_q_rows, n_real=n_real)
    return _mm_residual(o.reshape(nseq * lp, hq * hd), p['gqa_wo'][j].astype(BF16), h, 1.0, cfg,
                        hq * hd, cfg.out_cols // 2)


def _forward(cfg, x_prompt, x_sample, p):
    n_real = x_prompt.shape[1]
    assert x_sample.shape[1] == n_real and cfg.n_meta <= LANE and n_real % cfg.grid_w == 0
    d = cfg.d_model
    n_prompt, n_sample = x_prompt.shape[0], x_sample.shape[0]
    nseq = n_prompt + n_sample
    lp = n_real + LANE
    x = jnp.concatenate([x_prompt, x_sample], axis=0)
    meta = jnp.broadcast_to(p['meta_tokens'].astype(F32)[None], (nseq, cfg.n_meta, d))
    h = jnp.concatenate([x, meta, jnp.zeros((nseq, lp - n_real - cfg.n_meta, d), F32)], axis=1)
    h = h.reshape(nseq * lp, d)
    mla_tab, gqa_tab = _rope_tables(cfg, n_real, lp)
    ffn1 = _ffn_weights(p['ffn1_w_gate'], p['ffn1_w_up'], p['ffn1_w_down'], cfg)
    ffn2 = _ffn_weights(p['ffn2_w_gate'], p['ffn2_w_up'], p['ffn2_w_down'], cfg)

    for i in range(cfg.depth):
        h = _ffn(h, p['norm_ffn1'][i], ffn1[0][i], ffn1[1][i], ffn1[2][i], cfg)
        pi = dict(p, norm_mix_i=p['norm_mix'][i])
        if i % 2 == 0:
            h = _mla_layer(h, pi, i // 2, mla_tab, cfg, nseq, lp, n_real)
        else:
            h = _gqa_layer(h, pi, i // 2, gqa_tab, cfg, nseq, lp, n_real)
        h = _ffn(h, p['norm_ffn2'][i], ffn2[0][i], ffn2[1][i], ffn2[2][i], cfg)

    h3 = h.reshape(nseq, lp, d)
    y_prompt = _final_norm(h3, p['final_norm'], cfg, 0, n_prompt, n_real)
    y_sample = _final_norm(h3, p['final_norm'], cfg, n_prompt, n_sample, n_real)
    return y_prompt, y_sample


def kernel(x_prompt, x_sample, meta_tokens, norm_ffn1, ffn1_w_gate, ffn1_w_up, ffn1_w_down,
           norm_mix, norm_ffn2, ffn2_w_gate, ffn2_w_up, ffn2_w_down,
           mla_wq_a, mla_q_norm, mla_wq_b, mla_wkv_a, mla_kv_norm, mla_wkv_b, mla_wo,
           gqa_wqkv, gqa_q_norm, gqa_k_norm, gqa_wo, final_norm):
    p = dict(meta_tokens=meta_tokens, norm_ffn1=norm_ffn1, ffn1_w_gate=ffn1_w_gate,
             ffn1_w_up=ffn1_w_up, ffn1_w_down=ffn1_w_down, norm_mix=norm_mix,
             norm_ffn2=norm_ffn2, ffn2_w_gate=ffn2_w_gate, ffn2_w_up=ffn2_w_up,
             ffn2_w_down=ffn2_w_down, mla_wq_a=mla_wq_a, mla_q_norm=mla_q_norm,
             mla_wq_b=mla_wq_b, mla_wkv_a=mla_wkv_a, mla_kv_norm=mla_kv_norm,
             mla_wkv_b=mla_wkv_b, mla_wo=mla_wo, gqa_wqkv=gqa_wqkv, gqa_q_norm=gqa_q_norm,
             gqa_k_norm=gqa_k_norm, gqa_wo=gqa_wo, final_norm=final_norm)
    return _forward(Config(), x_prompt, x_sample, p)
```
